```python
import jax, jax.numpy as jnp
from jax import lax
import numpy as np

D_MODEL = 4096
BATCH = 1
SEQ = 8192
DEPTH = 1

N_MEM = 256
MEM_HEADS = 4
MEM_WIDTH = D_MODEL // 4
MEM_HEAD_DIM = MEM_WIDTH // MEM_HEADS
SB_HEAD_DIM = 128
SB_WIDTH = 3 * D_MODEL // 8
SB_HEADS = SB_WIDTH // SB_HEAD_DIM
SB_BLOCK = 128
RW_HEAD_DIM = 64
RW_WIDTH = 3 * D_MODEL // 8
RW_HEADS = RW_WIDTH // RW_HEAD_DIM
RW_DECAY_LORA = 128
RW_ICLR_LORA = 128
RW_GATE_LORA = 480
RW_SEG = 3 * RW_WIDTH + RW_DECAY_LORA + RW_ICLR_LORA + RW_GATE_LORA
RW_SPLITS = [RW_WIDTH, 2 * RW_WIDTH, 3 * RW_WIDTH,
             3 * RW_WIDTH + RW_DECAY_LORA,
             3 * RW_WIDTH + RW_DECAY_LORA + RW_ICLR_LORA]
RW_GN_EPS = 64e-5
N_BRANCHES = 3
IN_SPLITS = [SB_WIDTH, 2 * SB_WIDTH, 3 * SB_WIDTH,
             3 * SB_WIDTH + RW_SEG,
             3 * SB_WIDTH + RW_SEG + MEM_WIDTH]
IN_COLS = 3 * SB_WIDTH + RW_SEG + MEM_WIDTH + N_BRANCHES * D_MODEL
D_FF = -(-8 * D_MODEL // (3 * 256)) * 256
RMS_EPS = 1e-6

kernel_name = "hybrid_stickbreak_rwkv7_memxattn_gated"


def rms_norm(x, g, eps=RMS_EPS):
    xf = x.astype(jnp.float32)
    y = xf * lax.rsqrt(jnp.mean(xf * xf, axis=-1, keepdims=True) + eps)
    return (y * g.astype(jnp.float32)).astype(x.dtype)


def token_shift(c, mix):
    prev = jnp.pad(c, ((0, 0), (1, 0), (0, 0)))[:, :-1]
    return c + (prev - c) * mix


def stick_breaking_attention(q, k, v):
    B, T, H, hd = q.shape
    nb = T // SB_BLOCK
    scale = hd ** -0.5
    qb = jnp.moveaxis(q.reshape(B, nb, SB_BLOCK, H, hd), 1, 0)
    kf = k.astype(jnp.float32)
    vf = v.astype(jnp.float32)
    key_pos = jnp.arange(T)

    def block(args):
        q_blk, i = args
        z = jnp.einsum('bqhd,bkhd->bhqk', q_blk.astype(jnp.float32), kf) * scale
        q_pos = i * SB_BLOCK + jnp.arange(SB_BLOCK)
        causal = key_pos[None, :] < q_pos[:, None]
        sp = jnp.where(causal, jax.nn.softplus(z), 0.0)
        tail = lax.cumsum(sp, axis=3, reverse=True)
        weights = jnp.exp(jnp.where(causal, z - tail, -jnp.inf))
        return jnp.einsum('bhqk,bkhd->bqhd', weights, vf)

    ob = lax.map(block, (qb, jnp.arange(nb)))
    return jnp.moveaxis(ob, 0, 1).reshape(B, T, H, hd).astype(q.dtype)


def rwkv7_time_mix(seg, mix, w0, w_up, a0, a_up, g_up, k_k, k_a, r_k, ln_g, ln_b):
    dt = seg.dtype
    seg = token_shift(seg, mix).astype(jnp.float32)
    r, k, v, wd, ad, gd = jnp.split(seg, RW_SPLITS, axis=-1)
    B, T, _ = r.shape
    f32 = lambda t: t.astype(jnp.float32)
    w_log = -jax.nn.softplus(-(f32(w0) + jnp.tanh(wd) @ f32(w_up))) - 0.5
    decay = jnp.exp(-jnp.exp(w_log))
    a = jax.nn.sigmoid(f32(a0) + ad @ f32(a_up))
    g = jax.nn.sigmoid(gd) @ f32(g_up)
    heads = lambda t: t.reshape(B, T, RW_HEADS, RW_HEAD_DIM)
    kk = heads(k * f32(k_k))
    kk = kk / jnp.maximum(jnp.sqrt(jnp.sum(kk * kk, axis=-1, keepdims=True)), 1e-12)
    k = k * (1.0 + (a - 1.0) * f32(k_a))
    r, k, v, a, decay = heads(r), heads(k), heads(v), heads(a), heads(decay)

    def step(S, inp):
        r_t, w_t, k_t, v_t, kk_t, a_t = inp
        sa = jnp.einsum('bhvk,bhk->bhv', S, -kk_t)
        S = (S * w_t[:, :, None, :]
             + jnp.einsum('bhv,bhk->bhvk', sa, kk_t * a_t)
             + jnp.einsum('bhv,bhk->bhvk', v_t, k_t))
        y_t = jnp.einsum('bhvk,bhk->bhv', S, r_t)
        return S, y_t

    tm = lambda t: jnp.moveaxis(t, 1, 0)
    S0 = jnp.zeros((B, RW_HEADS, RW_HEAD_DIM, RW_HEAD_DIM), jnp.float32)
    _, y = lax.scan(step, S0, (tm(r), tm(decay), tm(k), tm(v), tm(kk), tm(a)))
    y = jnp.moveaxis(y, 0, 1)
    mu = jnp.mean(y, axis=-1, keepdims=True)
    var = jnp.mean(jnp.square(y - mu), axis=-1, keepdims=True)
    y = ((y - mu) * lax.rsqrt(var + RW_GN_EPS)).reshape(B, T, RW_WIDTH) * f32(ln_g) + f32(ln_b)
    bonus = jnp.sum(r * k * f32(r_k), axis=-1, keepdims=True) * v
    y = y + bonus.reshape(B, T, RW_WIDTH)
    return (y * g).astype(dt)


def memory_cross_attention(q, mem_n, w_kv, q_g, k_g):
    B, T, _ = q.shape
    M = mem_n.shape[1]
    mk, mv = jnp.split(mem_n @ w_kv, 2, axis=-1)
    q = rms_norm(q.reshape(B, T, MEM_HEADS, MEM_HEAD_DIM), q_g)
    mk = rms_norm(mk.reshape(B, M, MEM_HEADS, MEM_HEAD_DIM), k_g)
    mv = mv.reshape(B, M, MEM_HEADS, MEM_HEAD_DIM)
    s = jnp.einsum('bthd,bmhd->bhtm', q.astype(jnp.float32), mk.astype(jnp.float32)) * MEM_HEAD_DIM ** -0.5
    p = jax.nn.softmax(s, axis=-1)
    o = jnp.einsum('bhtm,bmhd->bthd', p, mv.astype(jnp.float32))
    return o.reshape(B, T, MEM_WIDTH).astype(q.dtype)


def setup_inputs(seed: int = 0) -> dict:
    key = jax.random.key(seed)
    ks = iter(jax.random.split(key, 32))
    L = DEPTH
    f = jnp.float32

    def nrm(shape, fan_in):
        return jax.random.normal(next(ks), shape, f) * fan_in ** -0.5

    def gain(shape):
        return 1.0 + 0.02 * jax.random.normal(next(ks), shape, f)

    def small(shape, s):
        return s * jax.random.normal(next(ks), shape, f)

    return {
        "x": jax.random.normal(next(ks), (BATCH, SEQ, D_MODEL), f),
        "mem": jax.random.normal(next(ks), (BATCH, N_MEM, D_MODEL), f),
        "attn_norm_g": gain((L, D_MODEL)),
        "mem_norm_g": gain((L, D_MODEL)),
        "w_in": nrm((L, D_MODEL, IN_COLS), D_MODEL),
        "sb_q_norm_g": gain((L, SB_HEAD_DIM)),
        "sb_k_norm_g": gain((L, SB_HEAD_DIM)),
        "rw_mix": jax.random.uniform(next(ks), (L, RW_SEG), f),
        "rw_w0": jax.random.uniform(next(ks), (L, RW_WIDTH), f, minval=-6.0, maxval=-1.0),
        "rw_w_up": nrm((L, RW_DECAY_LORA, RW_WIDTH), RW_DECAY_LORA),
        "rw_a0": small((L, RW_WIDTH), 0.1),
        "rw_a_up": nrm((L, RW_ICLR_LORA, RW_WIDTH), RW_ICLR_LORA),
        "rw_g_up": nrm((L, RW_GATE_LORA, RW_WIDTH), RW_GATE_LORA),
        "rw_k_k": 0.85 + small((L, RW_WIDTH), 0.02),
        "rw_k_a": gain((L, RW_WIDTH)),
        "rw_r_k": small((L, RW_HEADS, RW_HEAD_DIM), 0.1),
        "rw_ln_g": gain((L, RW_WIDTH)),
        "rw_ln_b": small((L, RW_WIDTH), 0.02),
        "mem_w_kv": nrm((L, D_MODEL, 2 * MEM_WIDTH), D_MODEL),
        "mem_q_norm_g": gain((L, MEM_HEAD_DIM)),
        "mem_k_norm_g": gain((L, MEM_HEAD_DIM)),
        "w_sb_o": nrm((L, SB_WIDTH, D_MODEL), SB_WIDTH),
        "w_rw_o": nrm((L, RW_WIDTH, D_MODEL), RW_WIDTH),
        "w_mem_o": nrm((L, MEM_WIDTH, D_MODEL), MEM_WIDTH),
        "w_out": nrm((L, D_MODEL, D_MODEL), D_MODEL),
        "ffn_norm_g": gain((L, D_MODEL)),
        "w_gate": nrm((L, D_MODEL, D_FF), D_MODEL),
        "w_up": nrm((L, D_MODEL, D_FF), D_MODEL),
        "w_down": nrm((L, D_FF, D_MODEL), D_FF),
    }


def reference(x, mem, attn_norm_g, mem_norm_g, w_in, sb_q_norm_g, sb_k_norm_g,
              rw_mix, rw_w0, rw_w_up, rw_a0, rw_a_up, rw_g_up, rw_k_k, rw_k_a,
              rw_r_k, rw_ln_g, rw_ln_b, mem_w_kv, mem_q_norm_g, mem_k_norm_g,
              w_sb_o, w_rw_o, w_mem_o, w_out, ffn_norm_g, w_gate, w_up, w_down):
    B, T, _ = x.shape
    h = x
    for l in range(DEPTH):
        xn = rms_norm(h, attn_norm_g[l])
        p = xn @ w_in[l]
        sb_q, sb_k, sb_v, rw_seg, mem_q, gates = jnp.split(p, IN_SPLITS, axis=-1)

        sb_heads = lambda t: t.reshape(B, T, SB_HEADS, SB_HEAD_DIM)
        q = rms_norm(sb_heads(sb_q), sb_q_norm_g[l])
        k = rms_norm(sb_heads(sb_k), sb_k_norm_g[l])
        o_sb = stick_breaking_attention(q, k, sb_heads(sb_v)).reshape(B, T, SB_WIDTH)
        u_sb = o_sb @ w_sb_o[l]

        o_rw = rwkv7_time_mix(rw_seg, rw_mix[l], rw_w0[l], rw_w_up[l], rw_a0[l], rw_a_up[l],
                              rw_g_up[l], rw_k_k[l], rw_k_a[l], rw_r_k[l], rw_ln_g[l], rw_ln_b[l])
        u_rw = o_rw @ w_rw_o[l]

        mem_n = rms_norm(mem, mem_norm_g[l])
        o_mem = memory_cross_attention(mem_q, mem_n, mem_w_kv[l], mem_q_norm_g[l], mem_k_norm_g[l])
        u_mem = o_mem @ w_mem_o[l]

        g_sb, g_rw, g_mem = jnp.split(jax.nn.sigmoid(gates), N_BRANCHES, axis=-1)
        merged = g_sb * u_sb + g_rw * u_rw + g_mem * u_mem
        h = h + merged @ w_out[l]

        hn = rms_norm(h, ffn_norm_g[l])
        h = h + (jax.nn.silu(hn @ w_gate[l]) * (hn @ w_up[l])) @ w_down[l]
    return h
```

```python
import functools

import jax
import jax.numpy as jnp
from jax import lax
from jax.experimental import pallas as pl
from jax.experimental.pallas import tpu as pltpu

F32 = jnp.float32
BF16 = jnp.bfloat16

LANES = 128
V7X_SCOPED_VMEM_CAP = 60000 * 1024

D_MODEL = 4096
SB_HEAD_DIM = 128
SB_WIDTH = 1536
SB_HEADS = SB_WIDTH // SB_HEAD_DIM
RW_HEAD_DIM = 64
RW_WIDTH = 1536
RW_PAIRS = RW_WIDTH // LANES
RW_LORA_W = 128
RW_LORA_A = 128
RW_LORA_G = 480
RW_LORA_G_PAD = 512
RW_LORA = RW_LORA_W + RW_LORA_A + RW_LORA_G
RW_LORA_PAD = RW_LORA_W + RW_LORA_A + RW_LORA_G_PAD
RW_SEG = 3 * RW_WIDTH + RW_LORA
RW_CHUNK = 64
RW_GN_EPS = 64e-5
MEM_WIDTH = 1024
MEM_HEADS = 4
MEM_HEAD_DIM = 256
D_FF = 11008
D_FF_PAD = 11264
RMS_EPS = 1e-6

COL_RW = 3 * SB_WIDTH
COL_LORA = COL_RW + 3 * RW_WIDTH
COL_MEMQ = COL_RW + RW_SEG
COL_GATES = COL_MEMQ + MEM_WIDTH


def _params(semantics, vmem_bytes):
    limit = min(int(vmem_bytes * 1.25) + (4 << 20), V7X_SCOPED_VMEM_CAP)
    return pltpu.CompilerParams(dimension_semantics=semantics, vmem_limit_bytes=limit)


def _nbytes(shape, dtype):
    n = 1
    for s in shape:
        n *= s
    return n * jnp.dtype(dtype).itemsize


def _sigmoid(x):
    return 1.0 / (1.0 + jnp.exp(-x))


def _softplus(x):
    return jnp.maximum(x, 0.0) + jnp.log1p(jnp.exp(-jnp.abs(x)))


def _dot(a, b):
    return jnp.dot(a, b, preferred_element_type=F32)


def _dot_nt(a, b):
    return lax.dot_general(a, b, (((1,), (1,)), ((), ())), preferred_element_type=F32)


def _dot_tn(a, b):
    return lax.dot_general(a, b, (((0,), (0,)), ((), ())), preferred_element_type=F32)


def _rmsnorm_kernel(x_ref, g_ref, o_ref):
    x = x_ref[...]
    ms = jnp.mean(x * x, axis=-1, keepdims=True)
    o_ref[...] = (x * lax.rsqrt(ms + RMS_EPS) * g_ref[...]).astype(o_ref.dtype)


def _rmsnorm(x, g, *, tm):
    m, d = x.shape
    vmem = 2 * (_nbytes((tm, d), F32) + _nbytes((tm, d), BF16))
    return pl.pallas_call(
        _rmsnorm_kernel,
        out_shape=jax.ShapeDtypeStruct((m, d), BF16),
        grid=(m // tm,),
        in_specs=[pl.BlockSpec((tm, d), lambda i: (i, 0)),
                  pl.BlockSpec((1, d), lambda i: (0, 0))],
        out_specs=pl.BlockSpec((tm, d), lambda i: (i, 0)),
        compiler_params=_params(("parallel",), vmem),
        name="rmsnorm",
    )(x, g.reshape(1, d))


def _headnorm(acc, gain, hd):
    outs = []
    for h in range(acc.shape[1] // hd):
        a = acc[:, h * hd:(h + 1) * hd]
        ms = jnp.mean(a * a, axis=-1, keepdims=True)
        outs.append(a * lax.rsqrt(ms + RMS_EPS) * gain[:, h * hd:(h + 1) * hd])
    return jnp.concatenate(outs, axis=1)


def _mm_kernel(x_ref, w_ref, *rest, epilogue, hd):
    o_ref = rest[-1]
    acc = _dot(x_ref[...], w_ref[...])
    if epilogue == "headnorm":
        acc = _headnorm(acc, rest[0][...], hd)
    elif epilogue == "sigmoid":
        acc = _sigmoid(acc)
    elif epilogue == "residual":
        acc = rest[0][...] + acc
    o_ref[...] = acc.astype(o_ref.dtype)


def _matmul(x, w, *, tm, tn, out_dtype, epilogue="none", extra=None, hd=0, name):
    m, k = x.shape
    n = w.shape[1]
    in_specs = [pl.BlockSpec((tm, k), lambda i, j: (i, 0)),
                pl.BlockSpec((k, tn), lambda i, j: (0, j))]
    args = [x, w]
    vmem = 2 * (_nbytes((tm, k), x.dtype) + _nbytes((k, tn), w.dtype) + _nbytes((tm, tn), out_dtype))
    vmem += _nbytes((tm, tn), F32)
    if epilogue == "headnorm":
        in_specs.append(pl.BlockSpec((1, tn), lambda i, j: (0, j)))
        args.append(extra)
    elif epilogue == "residual":
        in_specs.append(pl.BlockSpec((tm, tn), lambda i, j: (i, j)))
        args.append(extra)
        vmem += 2 * _nbytes((tm, tn), F32)
    return pl.pallas_call(
        functools.partial(_mm_kernel, epilogue=epilogue, hd=hd),
        out_shape=jax.ShapeDtypeStruct((m, n), out_dtype),
        grid=(m // tm, n // tn),
        in_specs=in_specs,
        out_specs=pl.BlockSpec((tm, tn), lambda i, j: (i, j)),
        compiler_params=_params(("parallel", "parallel"), vmem),
        name=name,
    )(*args)


def _sb_kernel(q_ref, k_ref, v_ref, tri_ref, o_ref, *, tq, tk):
    i = pl.program_id(1)
    q = q_ref[...]
    tri = tri_ref[...]

    def block(j, carry, acc, mask):
        start = pl.multiple_of(j * tk, tk)
        k = k_ref[pl.ds(start, tk), :]
        v = v_ref[pl.ds(start, tk), :]
        z = _dot_nt(q, k)
        sp = _softplus(z)
        if mask is not None:
            sp = jnp.where(mask, sp, 0.0)
        c = _dot(sp.astype(BF16), tri)
        w = jnp.exp(z - c - carry)
        if mask is not None:
            w = jnp.where(mask, w, 0.0)
        acc = acc + _dot(w.astype(BF16), v)
        carry = carry + c[:, 0:1]
        return carry, acc

    row = lax.broadcasted_iota(jnp.int32, (tq, tk), 0)
    col = lax.broadcasted_iota(jnp.int32, (tq, tk), 1)
    carry0 = jnp.zeros((tq, 1), F32)
    acc0 = jnp.zeros((tq, SB_HEAD_DIM), F32)
    carry, acc = block(i, carry0, acc0, col < row)

    def body(s, state):
        return block(i - 1 - s, state[0], state[1], None)

    carry, acc = lax.fori_loop(0, i, body, (carry, acc))
    o_ref[...] = acc.astype(o_ref.dtype)


def _sb_attention(qk, v, *, tq):
    t = qk.shape[0]
    tk = tq
    tri = jnp.tril(jnp.ones((tk, tk), F32)).astype(BF16)
    vmem = 2 * (2 * _nbytes((t, SB_HEAD_DIM), BF16) + 2 * _nbytes((tq, SB_HEAD_DIM), BF16)
                + _nbytes((tk, tk), BF16)) + 8 * _nbytes((tq, tk), F32)
    return pl.pallas_call(
        functools.partial(_sb_kernel, tq=tq, tk=tk),
        out_shape=jax.ShapeDtypeStruct((t, SB_WIDTH), BF16),
        grid=(SB_HEADS, t // tq),
        in_specs=[pl.BlockSpec((tq, SB_HEAD_DIM), lambda h, i: (i, h)),
                  pl.BlockSpec((t, SB_HEAD_DIM), lambda h, i: (0, SB_HEADS + h)),
                  pl.BlockSpec((t, SB_HEAD_DIM), lambda h, i: (0, h)),
                  pl.BlockSpec((tk, tk), lambda h, i: (0, 0))],
        out_specs=pl.BlockSpec((tq, SB_HEAD_DIM), lambda h, i: (i, h)),
        compiler_params=_params(("parallel", "parallel"), vmem),
        name="sb_attention",
    )(qk, qk, v, tri)


def _mem_attn_kernel(q_ref, k_ref, v_ref, o_ref):
    outs = []
    for h in range(MEM_HEADS):
        sl = slice(h * MEM_HEAD_DIM, (h + 1) * MEM_HEAD_DIM)
        s = _dot_nt(q_ref[:, sl], k_ref[:, sl])
        e = jnp.exp(s - jnp.max(s, axis=-1, keepdims=True))
        den = jnp.sum(e, axis=-1, keepdims=True)
        outs.append(_dot(e.astype(BF16), v_ref[:, sl]) / den)
    o_ref[...] = jnp.concatenate(outs, axis=1).astype(o_ref.dtype)


def _mem_attention(q, mk, mv, *, tm):
    t = q.shape[0]
    n_mem = mk.shape[0]
    vmem = 2 * (2 * _nbytes((tm, MEM_WIDTH), BF16) + 2 * _nbytes((n_mem, MEM_WIDTH), BF16))
    vmem += 4 * _nbytes((tm, n_mem), F32)
    return pl.pallas_call(
        _mem_attn_kernel,
        out_shape=jax.ShapeDtypeStruct((t, MEM_WIDTH), BF16),
        grid=(t // tm,),
        in_specs=[pl.BlockSpec((tm, MEM_WIDTH), lambda i: (i, 0)),
                  pl.BlockSpec((n_mem, MEM_WIDTH), lambda i: (0, 0)),
                  pl.BlockSpec((n_mem, MEM_WIDTH), lambda i: (0, 0))],
        out_specs=pl.BlockSpec((tm, MEM_WIDTH), lambda i: (i, 0)),
        compiler_params=_params(("parallel",), vmem),
        name="mem_attention",
    )(q, mk, mv)


def _split3(x):
    hi = x.astype(BF16)
    r1 = x - hi.astype(F32)
    mid = r1.astype(BF16)
    lo = (r1 - mid.astype(F32)).astype(BF16)
    return hi, mid, lo


def _group_sum(x, gmat):
    rows = jnp.concatenate([x[:, p * LANES:(p + 1) * LANES] for p in range(RW_PAIRS)], axis=0)
    hi = rows.astype(BF16)
    lo = (rows - hi.astype(F32)).astype(BF16)
    s = _dot(hi, gmat) + _dot(lo, gmat)
    c = RW_CHUNK
    return jnp.concatenate([s[p * c:(p + 1) * c] for p in range(RW_PAIRS)], axis=1)


def _shift_rows(x, prev_row):
    rolled = pltpu.roll(x, 1, axis=0)
    row = lax.broadcasted_iota(jnp.int32, x.shape, 0)
    return jnp.where(row == 0, prev_row, rolled)


def _rwkv_kernel(rkv_ref, lora_ref, mix_rkv_ref, mix_lora_ref, w0_ref, w_up_ref, a0_ref, a_up_ref,
                 g_up_ref, kk_scale_ref, ka_ref, rk_ref, lng_ref, lnb_ref, o_ref,
                 h_ref, prev_rkv_ref, prev_lora_ref,
                 at_s, rt_s, bh_s, kh_s, bg_s, kg_s, v_s, gc_s, y_s):
    c_idx = pl.program_id(0)
    C = RW_CHUNK
    W = RW_WIDTH

    @pl.when(c_idx == 0)
    def _():
        h_ref[...] = jnp.zeros_like(h_ref)
        prev_rkv_ref[...] = jnp.zeros_like(prev_rkv_ref)
        prev_lora_ref[...] = jnp.zeros_like(prev_lora_ref)

    x = rkv_ref[...]
    xl = lora_ref[...]
    xs = x + (_shift_rows(x, prev_rkv_ref[...]) - x) * mix_rkv_ref[...]
    ls = xl + (_shift_rows(xl, prev_lora_ref[...]) - xl) * mix_lora_ref[...]
    prev_rkv_ref[...] = x[C - 1:C, :]
    prev_lora_ref[...] = xl[C - 1:C, :]
    r = xs[:, 0:W]
    k = xs[:, W:2 * W]
    v = xs[:, 2 * W:3 * W]
    wd = ls[:, 0:RW_LORA_W]
    ad = ls[:, RW_LORA_W:RW_LORA_W + RW_LORA_A]
    gd = ls[:, RW_LORA_W + RW_LORA_A:]

    w_log = -_softplus(-(w0_ref[...] + _dot(jnp.tanh(wd).astype(BF16), w_up_ref[...]))) - 0.5
    lw = -jnp.exp(w_log)
    a = _sigmoid(a0_ref[...] + _dot(ad.astype(BF16), a_up_ref[...]))
    g = _dot(_sigmoid(gd).astype(BF16), g_up_ref[...])

    lane = lax.broadcasted_iota(jnp.int32, (LANES, LANES), 1)
    rowi = lax.broadcasted_iota(jnp.int32, (LANES, LANES), 0)
    same = (rowi >> 6) == (lane >> 6)
    gmat = jnp.where(same, 1.0, 0.0).astype(BF16)

    kk = k * kk_scale_ref[...]
    kk = kk / jnp.maximum(jnp.sqrt(_group_sum(kk * kk, gmat)), 1e-12)
    kt = k * (1.0 + (a - 1.0) * ka_ref[...])

    tri = jnp.where(lax.broadcasted_iota(jnp.int32, (C, C), 0) >= lax.broadcasted_iota(jnp.int32, (C, C), 1),
                    1.0, 0.0).astype(BF16)
    hi, mid, lo = _split3(lw)
    cum = _dot(tri, hi) + _dot(tri, mid) + _dot(tri, lo)
    cum_last = cum[C - 1:C, :]
    e_in = jnp.exp(cum)
    e_out = jnp.exp(-cum)
    e_end = jnp.exp(cum_last - cum)
    beta = kk * a
    at = -kk * jnp.exp(cum - lw)
    rt = r * e_in
    bh = beta * e_out
    kh = kt * e_out
    bg = beta * e_end
    kg = kt * e_end
    gc = jnp.broadcast_to(jnp.exp(cum_last), (8, W))
    for p in range(RW_PAIRS):
        sl = slice(p * LANES, (p + 1) * LANES)
        at_s[p] = at[:, sl]
        rt_s[p] = rt[:, sl]
        bh_s[p] = bh[:, sl]
        kh_s[p] = kh[:, sl]
        bg_s[p] = bg[:, sl]
        kg_s[p] = kg[:, sl]
        v_s[p] = v[:, sl]
        gc_s[p] = gc[:, sl]

    even = lax.broadcasted_iota(jnp.int32, (C, LANES), 1) < RW_HEAD_DIM
    strict = same & ((lane & 63) < (rowi & 63))
    incl = same & ((lane & 63) <= (rowi & 63))
    eye = lane == rowi

    def stack(x2):
        return jnp.concatenate([jnp.where(even, x2, 0.0), jnp.where(even, 0.0, x2)], axis=0)

    def fold(xs_):
        return xs_[0:C] + xs_[C:2 * C]

    def pair(p, carry_):
        at2 = at_s[p]
        rt2 = rt_s[p]
        v2 = v_s[p]
        s_at = stack(at2).astype(BF16)
        s_v = stack(v2).astype(BF16)
        lhs = jnp.concatenate([s_at, stack(rt2).astype(BF16)], axis=0)
        bh2 = bh_s[p].astype(BF16)
        kh2 = kh_s[p].astype(BF16)
        rhs = jnp.concatenate([bh2, bh2, kh2, kh2], axis=0)
        gm = _dot_nt(lhs, rhs)
        l_ab = jnp.where(strict, gm[0:LANES, 0:LANES], 0.0)
        l_ak = jnp.where(strict, gm[0:LANES, LANES:], 0.0)
        m_rb = jnp.where(incl, gm[LANES:, 0:LANES], 0.0)
        m_rk = jnp.where(incl, gm[LANES:, LANES:], 0.0)
        inv = jnp.where(eye, 1.0, 0.0) + l_ab
        pw = l_ab
        for _ in range(5):
            pwb = pw.astype(BF16)
            pw = _dot(pwb, pwb)
            inv = inv + _dot(inv.astype(BF16), pw.astype(BF16))
        lak_v = _dot(l_ak.astype(BF16), s_v)
        wu = _dot(inv.astype(BF16), jnp.concatenate([s_at, lak_v.astype(BF16)], axis=1))
        wub = wu.astype(BF16)
        bot = jnp.concatenate([jnp.zeros_like(s_v), s_v], axis=1)
        qy = _dot(jnp.concatenate([m_rb, m_rk], axis=1).astype(BF16),
                  jnp.concatenate([wub, bot], axis=0))
        q2 = rt2 + fold(qy[:, 0:LANES])
        yi2 = fold(qy[:, LANES:])
        wu_n = fold(wu)
        lhs_t = jnp.concatenate([bg_s[p], kg_s[p]], axis=0).astype(BF16)
        rhs2 = jnp.concatenate([wu_n.astype(BF16),
                                jnp.concatenate([jnp.zeros_like(v2), v2], axis=1).astype(BF16)], axis=0)
        pz = _dot_tn(lhs_t, rhs2)
        gcb = jnp.broadcast_to(gc_s[p][0:1, :], (LANES, LANES))
        p2 = jnp.where(eye, gcb, 0.0) + jnp.where(same, pz[:, 0:LANES], 0.0)
        z2 = jnp.where(same, pz[:, LANES:], 0.0)
        hb = h_ref[p].astype(BF16)
        y_s[p] = _dot(q2.astype(BF16), hb) + yi2
        h_ref[p] = _dot(p2.astype(BF16), hb) + z2
        return carry_

    lax.fori_loop(0, RW_PAIRS, pair, 0)

    y = jnp.concatenate([y_s[p] for p in range(RW_PAIRS)], axis=1)
    inv_n = 1.0 / RW_HEAD_DIM
    mu = _group_sum(y, gmat) * inv_n
    yc = y - mu
    var = _group_sum(yc * yc, gmat) * inv_n
    yn = yc * lax.rsqrt(var + RW_GN_EPS) * lng_ref[...] + lnb_ref[...]
    bonus = _group_sum(r * kt * rk_ref[...], gmat) * v
    o_ref[...] = ((yn + bonus) * g).astype(o_ref.dtype)


def _rwkv(rkv, lora, mix_rkv, mix_lora, w0, w_up, a0, a_up, g_up, kk_scale, ka, rk, lng, lnb):
    t = rkv.shape[0]
    C = RW_CHUNK
    W = RW_WIDTH
    row = lambda n: pl.BlockSpec((1, n), lambda c: (0, 0))
    full = lambda a: pl.BlockSpec(a.shape, lambda c: (0, 0))
    slab = lambda dt: pltpu.VMEM((RW_PAIRS, C, LANES), dt)
    scratch = [pltpu.VMEM((RW_PAIRS, LANES, LANES), F32),
               pltpu.VMEM((1, 3 * W), F32), pltpu.VMEM((1, RW_LORA_PAD), F32),
               slab(F32), slab(F32), slab(F32), slab(F32), slab(F32), slab(F32), slab(F32),
               pltpu.VMEM((RW_PAIRS, 8, LANES), F32), slab(F32)]
    vmem = 2 * (_nbytes((C, 3 * W), F32) + _nbytes((C, RW_LORA_PAD), F32) + _nbytes((C, W), BF16))
    vmem += 2 * (2 * _nbytes((RW_LORA_W, W), BF16) + _nbytes((RW_LORA_G_PAD, W), BF16))
    vmem += _nbytes((RW_PAIRS, LANES, LANES), F32) + 9 * _nbytes((RW_PAIRS, C, LANES), F32)
    vmem += 24 * _nbytes((C, W), F32)
    return pl.pallas_call(
        _rwkv_kernel,
        out_shape=jax.ShapeDtypeStruct((t, W), BF16),
        grid=(t // C,),
        in_specs=[pl.BlockSpec((C, 3 * W), lambda c: (c, 0)),
                  pl.BlockSpec((C, RW_LORA_PAD), lambda c: (c, 0)),
                  row(3 * W), row(RW_LORA_PAD), row(W), full(w_up), row(W), full(a_up), full(g_up),
                  row(W), row(W), row(W), row(W), row(W)],
        out_specs=pl.BlockSpec((C, W), lambda c: (c, 0)),
        scratch_shapes=scratch,
        compiler_params=_params(("arbitrary",), vmem),
        name="rwkv7",
    )(rkv, lora, mix_rkv, mix_lora, w0, w_up, a0, a_up, g_up, kk_scale, ka, rk, lng, lnb)


def _merge_kernel(osb_ref, orw_ref, omem_ref, wsb_ref, wrw_ref, wmem_ref, gsb_ref, grw_ref, gmem_ref, o_ref):
    m = gsb_ref[...].astype(F32) * _dot(osb_ref[...], wsb_ref[...])
    m = m + grw_ref[...].astype(F32) * _dot(orw_ref[...], wrw_ref[...])
    m = m + gmem_ref[...].astype(F32) * _dot(omem_ref[...], wmem_ref[...])
    o_ref[...] = m.astype(o_ref.dtype)


def _merge(o_sb, o_rw, o_mem, w_sb, w_rw, w_mem, gates, *, tm, tn):
    t = o_sb.shape[0]
    nj = D_MODEL // tn
    act = lambda a: pl.BlockSpec((tm, a.shape[1]), lambda i, j: (i, 0))
    wgt = lambda a: pl.BlockSpec((a.shape[0], tn), lambda i, j: (0, j))
    gate = lambda b: pl.BlockSpec((tm, tn), lambda i, j: (i, b * nj + j))
    vmem = 2 * sum(_nbytes((tm, a.shape[1]), BF16) + _nbytes((a.shape[1], tn), BF16) for a in (o_sb, o_rw, o_mem))
    vmem += 2 * 4 * _nbytes((tm, tn), gates.dtype) + 2 * _nbytes((tm, tn), F32)
    return pl.pallas_call(
        _merge_kernel,
        out_shape=jax.ShapeDtypeStruct((t, D_MODEL), BF16),
        grid=(t // tm, nj),
        in_specs=[act(o_sb), act(o_rw), act(o_mem), wgt(w_sb), wgt(w_rw), wgt(w_mem),
                  gate(0), gate(1), gate(2)],
        out_specs=pl.BlockSpec((tm, tn), lambda i, j: (i, j)),
        compiler_params=_params(("parallel", "parallel"), vmem),
        name="gated_merge",
    )(o_sb, o_rw, o_mem, w_sb, w_rw, w_mem, gates, gates, gates)


def _ffn_up_kernel(x_ref, wg_ref, wu_ref, o_ref):
    x = x_ref[...]
    a = _dot(x, wg_ref[...])
    b = _dot(x, wu_ref[...])
    o_ref[...] = (a * _sigmoid(a) * b).astype(o_ref.dtype)


def _ffn_up(x, wg, wu, *, tm, tn):
    m, k = x.shape
    n = wg.shape[1]
    vmem = 2 * (_nbytes((tm, k), BF16) + 2 * _nbytes((k, tn), BF16) + _nbytes((tm, tn), BF16))
    vmem += 3 * _nbytes((tm, tn), F32)
    return pl.pallas_call(
        _ffn_up_kernel,
        out_shape=jax.ShapeDtypeStruct((m, n), BF16),
        grid=(m // tm, n // tn),
        in_specs=[pl.BlockSpec((tm, k), lambda i, j: (i, 0)),
                  pl.BlockSpec((k, tn), lambda i, j: (0, j)),
                  pl.BlockSpec((k, tn), lambda i, j: (0, j))],
        out_specs=pl.BlockSpec((tm, tn), lambda i, j: (i, j)),
        compiler_params=_params(("parallel", "parallel"), vmem),
        name="ffn_up",
    )(x, wg, wu)


def _ffn_down_kernel(x_ref, w_ref, res_ref, o_ref, acc_ref):
    kk = pl.program_id(2)

    @pl.when(kk == 0)
    def _():
        acc_ref[...] = res_ref[...]

    acc_ref[...] += _dot(x_ref[...], w_ref[...])

    @pl.when(kk == pl.num_programs(2) - 1)
    def _():
        o_ref[...] = acc_ref[...]


def _ffn_down(x, w, res, *, tm, tn, tk):
    m, k = x.shape
    n = w.shape[1]
    vmem = 2 * (_nbytes((tm, tk), BF16) + _nbytes((tk, tn), BF16) + 2 * _nbytes((tm, tn), F32))
    vmem += 2 * _nbytes((tm, tn), F32)
    return pl.pallas_call(
        _ffn_down_kernel,
        out_shape=jax.ShapeDtypeStruct((m, n), F32),
        grid=(m // tm, n // tn, k // tk),
        in_specs=[pl.BlockSpec((tm, tk), lambda i, j, kk: (i, kk)),
                  pl.BlockSpec((tk, tn), lambda i, j, kk: (kk, j)),
                  pl.BlockSpec((tm, tn), lambda i, j, kk: (i, j))],
        out_specs=pl.BlockSpec((tm, tn), lambda i, j, kk: (i, j)),
        scratch_shapes=[pltpu.VMEM((tm, tn), F32)],
        compiler_params=_params(("parallel", "parallel", "arbitrary"), vmem),
        name="ffn_down",
    )(x, w, res)


def _layer(h, mem, attn_norm_g, mem_norm_g, w_in, sb_q_norm_g, sb_k_norm_g,
           rw_mix, rw_w0, rw_w_up, rw_a0, rw_a_up, rw_g_up, rw_k_k, rw_k_a,
           rw_r_k, rw_ln_g, rw_ln_b, mem_w_kv, mem_q_norm_g, mem_k_norm_g,
           w_sb_o, w_rw_o, w_mem_o, w_out, ffn_norm_g, w_gate, w_up, w_down):
    bf = lambda a: a.astype(BF16)
    row = lambda a: a.reshape(1, -1).astype(F32)

    w_qk = bf(w_in[:, 0:2 * SB_WIDTH])
    w_v = bf(w_in[:, 2 * SB_WIDTH:COL_RW])
    w_rkv = bf(w_in[:, COL_RW:COL_LORA])
    w_lora = bf(jnp.pad(w_in[:, COL_LORA:COL_MEMQ], ((0, 0), (0, RW_LORA_PAD - RW_LORA))))
    w_mq = bf(w_in[:, COL_MEMQ:COL_GATES])
    w_gates = bf(w_in[:, COL_GATES:])
    ff_pad = D_FF_PAD - D_FF
    w_gate_b = bf(jnp.pad(w_gate, ((0, 0), (0, ff_pad))))
    w_up_b = bf(jnp.pad(w_up, ((0, 0), (0, ff_pad))))
    w_down_b = bf(jnp.pad(w_down, ((0, ff_pad), (0, 0))))

    xn = _rmsnorm(h, attn_norm_g, tm=256)
    qk_gain = jnp.concatenate([jnp.tile(sb_q_norm_g * SB_HEAD_DIM ** -0.5, SB_HEADS),
                               jnp.tile(sb_k_norm_g, SB_HEADS)]).reshape(1, -1)
    qk = _matmul(xn, w_qk, tm=1024, tn=512, out_dtype=BF16, epilogue="headnorm", extra=qk_gain,
                 hd=SB_HEAD_DIM, name="proj_qk")
    v_sb = _matmul(xn, w_v, tm=1024, tn=512, out_dtype=BF16, name="proj_v")
    rkv = _matmul(xn, w_rkv, tm=1024, tn=512, out_dtype=F32, name="proj_rkv")
    lora = _matmul(xn, w_lora, tm=1024, tn=RW_LORA_PAD, out_dtype=F32, name="proj_lora")
    mq_gain = jnp.tile(mem_q_norm_g * MEM_HEAD_DIM ** -0.5, MEM_HEADS).reshape(1, -1)
    mem_q = _matmul(xn, w_mq, tm=1024, tn=512, out_dtype=BF16, epilogue="headnorm", extra=mq_gain,
                    hd=MEM_HEAD_DIM, name="proj_memq")
    gates = _matmul(xn, w_gates, tm=1024, tn=512, out_dtype=BF16, epilogue="sigmoid", name="proj_gates")

    o_sb = _sb_attention(qk, v_sb, tq=256)

    mix_lora = jnp.pad(rw_mix[3 * RW_WIDTH:], (0, RW_LORA_PAD - RW_LORA)).reshape(1, -1)
    g_up_b = bf(jnp.pad(rw_g_up, ((0, RW_LORA_G_PAD - RW_LORA_G), (0, 0))))
    o_rw = _rwkv(rkv, lora, row(rw_mix[:3 * RW_WIDTH]), mix_lora, row(rw_w0), bf(rw_w_up), row(rw_a0),
                 bf(rw_a_up), g_up_b, row(rw_k_k), row(rw_k_a), row(rw_r_k), row(rw_ln_g), row(rw_ln_b))

    mem_n = _rmsnorm(mem, mem_norm_g, tm=mem.shape[0])
    mk_gain = jnp.tile(mem_k_norm_g, MEM_HEADS).reshape(1, -1)
    mk = _matmul(mem_n, bf(mem_w_kv[:, :MEM_WIDTH]), tm=mem.shape[0], tn=512, out_dtype=BF16,
                 epilogue="headnorm", extra=mk_gain, hd=MEM_HEAD_DIM, name="proj_memk")
    mv = _matmul(mem_n, bf(mem_w_kv[:, MEM_WIDTH:]), tm=mem.shape[0], tn=512, out_dtype=BF16, name="proj_memv")
    o_mem = _mem_attention(mem_q, mk, mv, tm=512)

    merged = _merge(o_sb, o_rw, o_mem, bf(w_sb_o), bf(w_rw_o), bf(w_mem_o), gates, tm=1024, tn=512)
    h1 = _matmul(merged, bf(w_out), tm=1024, tn=512, out_dtype=F32, epilogue="residual", extra=h, name="proj_out")

    hn = _rmsnorm(h1, ffn_norm_g, tm=256)
    act = _ffn_up(hn, w_gate_b, w_up_b, tm=1024, tn=512)
    return _ffn_down(act, w_down_b, h1, tm=1024, tn=1024, tk=1024)


def kernel(x, mem, attn_norm_g, mem_norm_g, w_in, sb_q_norm_g, sb_k_norm_g, rw_mix, rw_w0, rw_w_up, rw_a0, rw_a_up, rw_g_up, rw_k_k, rw_k_a, rw_r_k, rw_ln_g, rw_ln_b, mem_w_kv, mem_q_norm_g, mem_k_norm_g, w_sb_o, w_rw_o, w_mem_o, w_out, ffn_norm_g, w_gate, w_up, w_down):
    b, t, d = x.shape
    assert b == 1 and d == D_MODEL and w_in.shape[0] == 1
    h = x[0]
    out = _layer(h, mem[0], attn_norm_g[0], mem_norm_g[0], w_in[0], sb_q_norm_g[0], sb_k_norm_g[0],
                 rw_mix[0], rw_w0[0], rw_w_up[0], rw_a0[0], rw_a_up[0], rw_g_up[0], rw_k_k[0], rw_k_a[0],
                 rw_r_k[0], rw_ln_g[0], rw_ln_b[0], mem_w_kv[0], mem_q_norm_g[0], mem_k_norm_g[0],
                 w_sb_o[0], w_rw_o[0], w_mem_o[0], w_out[0], ffn_norm_g[0], w_gate[0], w_up[0], w_down[0])
    return out[None]
```
